```python
import jax, jax.numpy as jnp
from jax import lax
import numpy as np

D_MODEL = 2048
BATCH = 1
SEQ = 8192
DEPTH = 2

GRID_W = 64
CTX_LEN = 256
HEAD_DIM = 128
N_HEADS = D_MODEL // HEAD_DIM
NA_HEADS = N_HEADS // 2
NA_WIN_H = 8
NA_WIN_W = 16
SW_HEADS = N_HEADS - NA_HEADS
SW_KV_HEADS = 2
SW_RADIUS = 128
GQA_HEADS = N_HEADS
GQA_KV_HEADS = 4
BLOCK = 128
D_FF = ((8 * D_MODEL // 3 + 255) // 256) * 256
MACARON_WEIGHT = 0.5
ROPE_THETA = 10000.0
EPS = 1e-6
NEG_INF = -1e30
N_MOD = 9
N_EVEN = (DEPTH + 1) // 2
N_ODD = DEPTH // 2
AB_IN = (NA_HEADS + SW_HEADS + 2 * NA_HEADS + 2 * SW_KV_HEADS) * HEAD_DIM
C_IN = (GQA_HEADS + 2 * GQA_KV_HEADS) * HEAD_DIM
ATTN_SCALE = HEAD_DIM ** -0.5

kernel_name = 'hybrid_natten_swa_gqa_macaron_dit'


def rms_norm(x, g):
    xf = x.astype(jnp.float32)
    y = xf * lax.rsqrt(jnp.mean(xf * xf, axis=-1, keepdims=True) + EPS)
    return (y * g.astype(jnp.float32)).astype(x.dtype)


def modulate(x, g, shift, scale):
    return rms_norm(x, g) * (1 + scale) + shift


def heads(t, n):
    return t.reshape(t.shape[:-1] + (n, HEAD_DIM))


def swiglu(h, w_gate, w_up, w_down):
    return (jax.nn.silu(h @ w_gate) * (h @ w_up)) @ w_down


def axial_rope_tables(n_tokens):
    t = jnp.arange(n_tokens)
    row = (t // GRID_W).astype(jnp.float32)
    col = (t % GRID_W).astype(jnp.float32)
    axis_dim = HEAD_DIM // 2
    inv = ROPE_THETA ** (-jnp.arange(0, axis_dim, 2, dtype=jnp.float32) / axis_dim)
    ang = jnp.concatenate([row[:, None] * inv, col[:, None] * inv], axis=-1)
    return jnp.cos(ang), jnp.sin(ang)


def apply_rope(x, cos, sin):
    x1, x2 = jnp.split(x.astype(jnp.float32), 2, axis=-1)
    c = cos[None, :, None, :]
    s = sin[None, :, None, :]
    return jnp.concatenate([x1 * c - x2 * s, x1 * s + x2 * c], axis=-1).astype(x.dtype)


def ctx_self_attention(q, k, v, sink=None):
    B, L, Hq, d = q.shape
    Hk = k.shape[2]
    G = Hq // Hk
    qg = q.reshape(B, L, Hk, G, d)
    s = jnp.einsum('blkgd,bmkd->bkglm', qg, k).astype(jnp.float32) * ATTN_SCALE
    if sink is not None:
        sk = jnp.broadcast_to(sink.reshape(Hk, G)[None, :, :, None, None].astype(jnp.float32), s.shape[:-1] + (1,))
        s = jnp.concatenate([s, sk], axis=-1)
    p = jax.nn.softmax(s, axis=-1)[..., :L].astype(v.dtype)
    o = jnp.einsum('bkglm,bmkd->blkgd', p, v)
    return o.reshape(B, L, Hq * d)


def neighbourhood_attention(q, k, v, kc, vc, rel_bias):
    B, S, H, d = q.shape
    rows = S // GRID_W
    wh = min(NA_WIN_H, rows)
    ww = NA_WIN_W
    n_win = wh * ww
    col = jnp.arange(GRID_W)
    cs = jnp.clip(col - ww // 2, 0, GRID_W - ww)
    col_idx = cs[:, None] + jnp.arange(ww)[None, :]
    dc = col_idx - col[:, None]

    def row_block(r):
        rs = jnp.clip(r - wh // 2, 0, rows - wh)
        row_idx = rs + jnp.arange(wh)
        dr = row_idx - r
        tok = (row_idx[None, :, None] * GRID_W + col_idx[:, None, :]).reshape(GRID_W, n_win)
        kw = jnp.take(k, tok, axis=1)
        vw = jnp.take(v, tok, axis=1)
        qr = lax.dynamic_slice_in_dim(q, r * GRID_W, GRID_W, axis=1)
        bias = rel_bias[:, dr[None, :, None] + NA_WIN_H - 1, dc[:, None, :] + NA_WIN_W - 1]
        bias = bias.reshape(H, GRID_W, n_win).astype(jnp.float32)
        s_win = jnp.einsum('bqhd,bqnhd->bhqn', qr, kw).astype(jnp.float32) * ATTN_SCALE + bias[None]
        s_ctx = jnp.einsum('bqhd,bmhd->bhqm', qr, kc).astype(jnp.float32) * ATTN_SCALE
        p = jax.nn.softmax(jnp.concatenate([s_win, s_ctx], axis=-1), axis=-1).astype(v.dtype)
        return (jnp.einsum('bhqn,bqnhd->bqhd', p[..., :n_win], vw)
                + jnp.einsum('bhqm,bmhd->bqhd', p[..., n_win:], vc))

    out = lax.map(row_block, jnp.arange(rows))
    return out.transpose(1, 0, 2, 3, 4).reshape(B, S, H * d)


def sliding_window_attention(q, k, v, kc, vc, sink):
    B, S, Hq, d = q.shape
    Hk = k.shape[2]
    G = Hq // Hk
    L = kc.shape[1]
    nb = S // BLOCK
    pad = ((0, 0), (BLOCK, BLOCK), (0, 0), (0, 0))

    def bands(t):
        tp = jnp.pad(t, pad).reshape(B, nb + 2, BLOCK, Hk, d)
        return jnp.concatenate([tp[:, :-2], tp[:, 1:-1], tp[:, 2:]], axis=2)

    kb = bands(k)
    vb = bands(v)
    qb = q.reshape(B, nb, BLOCK, Hk, G, d)
    s_win = jnp.einsum('bnikgd,bnjkd->bnkgij', qb, kb).astype(jnp.float32) * ATTN_SCALE
    i = jnp.arange(BLOCK)[:, None]
    j = jnp.arange(3 * BLOCK)[None, :]
    rel = j - BLOCK - i
    kpos = jnp.arange(nb)[:, None] * BLOCK - BLOCK + jnp.arange(3 * BLOCK)[None, :]
    valid = (jnp.abs(rel) <= SW_RADIUS)[None] & ((kpos >= 0) & (kpos < S))[:, None, :]
    s_win = jnp.where(valid[None, :, None, None], s_win, NEG_INF)
    s_ctx = jnp.einsum('bnikgd,bmkd->bnkgim', qb, kc).astype(jnp.float32) * ATTN_SCALE
    s_sink = jnp.broadcast_to(sink.reshape(Hk, G)[None, None, :, :, None, None].astype(jnp.float32),
                              s_win.shape[:-1] + (1,))
    p = jax.nn.softmax(jnp.concatenate([s_win, s_ctx, s_sink], axis=-1), axis=-1).astype(v.dtype)
    nw = 3 * BLOCK
    o = (jnp.einsum('bnkgij,bnjkd->bnikgd', p[..., :nw], vb)
         + jnp.einsum('bnkgim,bmkd->bnikgd', p[..., nw:nw + L], vc))
    return o.reshape(B, S, Hq * d)


def dense_block_attention(q, k, v, kc, vc):
    B, S, Hq, d = q.shape
    Hk = k.shape[2]
    G = Hq // Hk
    nb = S // BLOCK
    kk = jnp.concatenate([k, kc], axis=1)
    vv = jnp.concatenate([v, vc], axis=1)
    qb = q.reshape(B, nb, BLOCK, Hk, G, d).transpose(1, 0, 2, 3, 4, 5)

    def blk(qi):
        s = jnp.einsum('bikgd,bjkd->bkgij', qi, kk).astype(jnp.float32) * ATTN_SCALE
        p = jax.nn.softmax(s, axis=-1).astype(vv.dtype)
        return jnp.einsum('bkgij,bjkd->bikgd', p, vv)

    o = lax.map(blk, qb)
    return o.transpose(1, 0, 2, 3, 4, 5).reshape(B, S, Hq * d)


def mixer_na_sw(h, hc, w_in, w_out, na_qg, na_kg, rel_bias, sw_qg, sw_kg, sink, cos, sin, with_ctx):
    qa_d = NA_HEADS * HEAD_DIM
    q_d = qa_d + SW_HEADS * HEAD_DIM
    kva = NA_HEADS * HEAD_DIM
    kvb = SW_KV_HEADS * HEAD_DIM

    def q_split(p):
        qa = rms_norm(heads(p[..., :qa_d], NA_HEADS), na_qg)
        qb = rms_norm(heads(p[..., qa_d:], SW_HEADS), sw_qg)
        return qa, qb

    def kv_split(p):
        ka = rms_norm(heads(p[..., :kva], NA_HEADS), na_kg)
        va = heads(p[..., kva:2 * kva], NA_HEADS)
        kb = rms_norm(heads(p[..., 2 * kva:2 * kva + kvb], SW_KV_HEADS), sw_kg)
        vb = heads(p[..., 2 * kva + kvb:], SW_KV_HEADS)
        return ka, va, kb, vb

    p = h @ w_in
    qa, qb = q_split(p[..., :q_d])
    ka, va, kb, vb = kv_split(p[..., q_d:])
    qb = apply_rope(qb, cos, sin)
    kb = apply_rope(kb, cos, sin)
    ka_c, va_c, kb_c, vb_c = kv_split(hc @ w_in[:, q_d:])
    o_a = neighbourhood_attention(qa, ka, va, ka_c, va_c, rel_bias)
    o_b = sliding_window_attention(qb, kb, vb, kb_c, vb_c, sink)
    out = jnp.concatenate([o_a, o_b], axis=-1) @ w_out
    out_c = None
    if with_ctx:
        qa_c, qb_c = q_split(hc @ w_in[:, :q_d])
        out_c = jnp.concatenate([ctx_self_attention(qa_c, ka_c, va_c),
                                 ctx_self_attention(qb_c, kb_c, vb_c, sink)], axis=-1) @ w_out
    return out, out_c


def mixer_gqa(h, hc, w_in, w_out, q_gain, k_gain, cos, sin, with_ctx):
    qd = GQA_HEADS * HEAD_DIM
    kd = GQA_KV_HEADS * HEAD_DIM
    p = h @ w_in
    q = apply_rope(rms_norm(heads(p[..., :qd], GQA_HEADS), q_gain), cos, sin)
    k = apply_rope(rms_norm(heads(p[..., qd:qd + kd], GQA_KV_HEADS), k_gain), cos, sin)
    v = heads(p[..., qd + kd:], GQA_KV_HEADS)
    pc = hc @ w_in[:, qd:]
    kc = rms_norm(heads(pc[..., :kd], GQA_KV_HEADS), k_gain)
    vc = heads(pc[..., kd:], GQA_KV_HEADS)
    out = dense_block_attention(q, k, v, kc, vc) @ w_out
    out_c = None
    if with_ctx:
        qc = rms_norm(heads(hc @ w_in[:, :qd], GQA_HEADS), q_gain)
        out_c = ctx_self_attention(qc, kc, vc) @ w_out
    return out, out_c


def setup_inputs(seed: int = 0) -> dict:
    key = jax.random.key(seed)
    ks = jax.random.split(key, 24)
    f32 = jnp.float32
    nrm = lambda k, shape, s: jax.random.normal(k, shape, f32) * s
    gain = lambda k, shape: 1.0 + 0.02 * jax.random.normal(k, shape, f32)
    return {
        'x': nrm(ks[0], (BATCH, SEQ, D_MODEL), 1.0),
        'c': nrm(ks[1], (BATCH, D_MODEL), 1.0),
        'ctx': nrm(ks[2], (BATCH, CTX_LEN, D_MODEL), 1.0),
        'c_ctx': nrm(ks[3], (D_MODEL,), 1.0),
        'adaln_w': nrm(ks[4], (DEPTH, D_MODEL, N_MOD * D_MODEL), 0.5 * D_MODEL ** -0.5),
        'adaln_b': nrm(ks[5], (DEPTH, N_MOD * D_MODEL), 0.02),
        'norm_g': gain(ks[6], (DEPTH, 3, D_MODEL)),
        'ffn_w_gate': nrm(ks[7], (DEPTH, 2, D_MODEL, D_FF), D_MODEL ** -0.5),
        'ffn_w_up': nrm(ks[8], (DEPTH, 2, D_MODEL, D_FF), D_MODEL ** -0.5),
        'ffn_w_down': nrm(ks[9], (DEPTH, 2, D_FF, D_MODEL), D_FF ** -0.5),
        'ab_w_in': nrm(ks[10], (N_EVEN, D_MODEL, AB_IN), D_MODEL ** -0.5),
        'ab_w_out': nrm(ks[11], (N_EVEN, N_HEADS * HEAD_DIM, D_MODEL), (N_HEADS * HEAD_DIM) ** -0.5),
        'na_q_gain': gain(ks[12], (N_EVEN, HEAD_DIM)),
        'na_k_gain': gain(ks[13], (N_EVEN, HEAD_DIM)),
        'na_rel_bias': nrm(ks[14], (N_EVEN, NA_HEADS, 2 * NA_WIN_H - 1, 2 * NA_WIN_W - 1), 0.1),
        'sw_q_gain': gain(ks[15], (N_EVEN, HEAD_DIM)),
        'sw_k_gain': gain(ks[16], (N_EVEN, HEAD_DIM)),
        'sw_sink': nrm(ks[17], (N_EVEN, SW_HEADS), 0.5),
        'gqa_w_in': nrm(ks[18], (N_ODD, D_MODEL, C_IN), D_MODEL ** -0.5),
        'gqa_w_out': nrm(ks[19], (N_ODD, GQA_HEADS * HEAD_DIM, D_MODEL), (GQA_HEADS * HEAD_DIM) ** -0.5),
        'gqa_q_gain': gain(ks[20], (N_ODD, HEAD_DIM)),
        'gqa_k_gain': gain(ks[21], (N_ODD, HEAD_DIM)),
    }


def reference(x, c, ctx, c_ctx, adaln_w, adaln_b, norm_g, ffn_w_gate, ffn_w_up, ffn_w_down,
              ab_w_in, ab_w_out, na_q_gain, na_k_gain, na_rel_bias, sw_q_gain, sw_k_gain, sw_sink,
              gqa_w_in, gqa_w_out, gqa_q_gain, gqa_k_gain):
    S = x.shape[1]
    cos, sin = axial_rope_tables(S)
    xc = ctx
    for i in range(DEPTH):
        with_ctx = i < DEPTH - 1
        m = [t[:, None, :] for t in jnp.split(jax.nn.silu(c) @ adaln_w[i] + adaln_b[i], N_MOD, axis=-1)]
        mc = jnp.split(jax.nn.silu(c_ctx) @ adaln_w[i] + adaln_b[i], N_MOD, axis=-1)
        wa = (ffn_w_gate[i, 0], ffn_w_up[i, 0], ffn_w_down[i, 0])
        x = x + MACARON_WEIGHT * m[2] * swiglu(modulate(x, norm_g[i, 0], m[0], m[1]), *wa)
        xc = xc + MACARON_WEIGHT * mc[2] * swiglu(modulate(xc, norm_g[i, 0], mc[0], mc[1]), *wa)
        h = modulate(x, norm_g[i, 1], m[3], m[4])
        hc = modulate(xc, norm_g[i, 1], mc[3], mc[4])
        if i % 2 == 0:
            e = i // 2
            out, out_c = mixer_na_sw(h, hc, ab_w_in[e], ab_w_out[e], na_q_gain[e], na_k_gain[e],
                                     na_rel_bias[e], sw_q_gain[e], sw_k_gain[e], sw_sink[e],
                                     cos, sin, with_ctx)
        else:
            o = i // 2
            out, out_c = mixer_gqa(h, hc, gqa_w_in[o], gqa_w_out[o], gqa_q_gain[o], gqa_k_gain[o],
                                   cos, sin, with_ctx)
        x = x + m[5] * out
        wb = (ffn_w_gate[i, 1], ffn_w_up[i, 1], ffn_w_down[i, 1])
        x = x + MACARON_WEIGHT * m[8] * swiglu(modulate(x, norm_g[i, 2], m[6], m[7]), *wb)
        if with_ctx:
            xc = xc + mc[5] * out_c
            xc = xc + MACARON_WEIGHT * mc[8] * swiglu(modulate(xc, norm_g[i, 2], mc[6], mc[7]), *wb)
    return x
```

```python
import functools

import numpy as np
import jax
import jax.numpy as jnp
from jax import lax
from jax.experimental import pallas as pl
from jax.experimental.pallas import tpu as pltpu

F32 = jnp.float32
BF16 = jnp.bfloat16

HEAD_DIM = 128
GRID_W = 64
NA_WIN_H = 8
NA_WIN_W = 16
SW_RADIUS = 128
BLOCK = 128
SW_KV_HEADS = 2
GQA_KV_HEADS = 4
N_MOD = 9
EPS = 1e-6
NEG_INF = -1e30
ROPE_THETA = 10000.0
MACARON_WEIGHT = 0.5
ATTN_SCALE = HEAD_DIM ** -0.5

V7X_VMEM_BYTES = 64 * 1024 * 1024
VMEM_LIMIT_BYTES = 60000 * 1024

ROW_CHUNK = 128
COL_CHUNK = 512

NA_Q_ROWS = 4
NA_K_ROWS = NA_Q_ROWS + NA_WIN_H - 1


def _params(n_axes):
    return pltpu.CompilerParams(dimension_semantics=("arbitrary",) * n_axes,
                                vmem_limit_bytes=VMEM_LIMIT_BYTES)


def _pick_tile(n, candidates):
    for t in candidates:
        if n % t == 0:
            return t
    raise ValueError(f"no tile for {n} among {candidates}")


def _adaln_kernel(c_ref, w_ref, b_ref, o_ref):
    d = w_ref.shape[0]
    tn = w_ref.shape[1]

    def body(kc, acc):
        a0, a1 = acc
        r = pl.multiple_of(kc * 8, 8)
        w = w_ref[pl.ds(r, 8), :]
        cv = c_ref[pl.ds(r, 8), :]
        s = cv * jax.nn.sigmoid(cv)
        return a0 + s[:, 0:1] * w, a1 + s[:, 1:2] * w

    z = jnp.zeros((8, tn), F32)
    a0, a1 = lax.fori_loop(0, d // 8, body, (z, z), unroll=8)
    o_ref[0:1, :] = jnp.sum(a0, axis=0, keepdims=True) + b_ref[...]
    o_ref[1:2, :] = jnp.sum(a1, axis=0, keepdims=True) + b_ref[...]


def _adaln(c_cols, adaln_w, adaln_b):
    depth, d, n = adaln_w.shape
    tn = _pick_tile(n, (1024, 512, 256, 128))
    return pl.pallas_call(
        _adaln_kernel,
        out_shape=jax.ShapeDtypeStruct((depth, 2, n), F32),
        grid=(depth, n // tn),
        in_specs=[
            pl.BlockSpec((d, 2), lambda l, j: (0, 0)),
            pl.BlockSpec((None, d, tn), lambda l, j: (l, 0, j)),
            pl.BlockSpec((None, 1, tn), lambda l, j: (l, 0, j)),
        ],
        out_specs=pl.BlockSpec((None, 2, tn), lambda l, j: (l, 0, j)),
        compiler_params=_params(2),
        name="adaln",
    )(c_cols, adaln_w, adaln_b.reshape(depth, 1, n))


def _is_ctx_rows(tm, n_latent, r0=0, nr=None):
    nr = tm if nr is None else nr
    row = pl.program_id(0) * tm + r0 + lax.broadcasted_iota(jnp.int32, (nr, 1), 0)
    return row >= n_latent


def _mod_row(mod_ref, is_ctx, idx, d):
    lat = mod_ref[0:1, idx * d:(idx + 1) * d]
    ctx = mod_ref[1:2, idx * d:(idx + 1) * d]
    return jnp.where(is_ctx, ctx, lat)


def _modulated(x, g, mod_ref, is_ctx, sub):
    d = x.shape[1]
    shift = _mod_row(mod_ref, is_ctx, 3 * sub, d)
    scale = _mod_row(mod_ref, is_ctx, 3 * sub + 1, d)
    y = x * lax.rsqrt(jnp.mean(x * x, axis=-1, keepdims=True) + EPS)
    return (y * g) * (1.0 + scale) + shift


def _ffn_kernel(x_ref, mod_ref, g_ref, wg_ref, wu_ref, wd_ref, o_ref, h_ref, *, sub, n_latent):
    j = pl.program_id(1)
    tm, d = x_ref.shape
    row_chunks = [(r0, min(ROW_CHUNK, tm - r0)) for r0 in range(0, tm, ROW_CHUNK)]

    @pl.when(j == 0)
    def _():
        for r0, nr in row_chunks:
            is_ctx = _is_ctx_rows(tm, n_latent, r0, nr)
            h_ref[r0:r0 + nr, :] = _modulated(x_ref[r0:r0 + nr, :], g_ref[...], mod_ref, is_ctx, sub).astype(BF16)
        o_ref[...] = jnp.zeros_like(o_ref)

    h = h_ref[...]
    gate = jnp.dot(h, wg_ref[...], preferred_element_type=F32)
    up = jnp.dot(h, wu_ref[...], preferred_element_type=F32)
    a = (gate * jax.nn.sigmoid(gate) * up).astype(BF16)
    for c0 in range(0, d, COL_CHUNK):
        o_ref[:, c0:c0 + COL_CHUNK] += jnp.dot(a, wd_ref[:, c0:c0 + COL_CHUNK], preferred_element_type=F32)

    @pl.when(j == pl.num_programs(1) - 1)
    def _():
        for r0, nr in row_chunks:
            is_ctx = _is_ctx_rows(tm, n_latent, r0, nr)
            res_gate = _mod_row(mod_ref, is_ctx, 3 * sub + 2, d)
            o_ref[r0:r0 + nr, :] = x_ref[r0:r0 + nr, :] + MACARON_WEIGHT * res_gate * o_ref[r0:r0 + nr, :]


def _ffn(x, n_rows, mod, g, wg, wu, wd, *, sub, n_latent):
    d = x.shape[1]
    f = wg.shape[1]
    tm = _pick_tile(n_rows, (768, 512, 256, 128))
    tf = _pick_tile(f, (512, 256, 128))
    return pl.pallas_call(
        functools.partial(_ffn_kernel, sub=sub, n_latent=n_latent),
        out_shape=jax.ShapeDtypeStruct((n_rows, d), F32),
        grid=(n_rows // tm, f // tf),
        in_specs=[
            pl.BlockSpec((tm, d), lambda i, j: (i, 0)),
            pl.BlockSpec(mod.shape, lambda i, j: (0, 0)),
            pl.BlockSpec((1, d), lambda i, j: (0, 0)),
            pl.BlockSpec((d, tf), lambda i, j: (0, j)),
            pl.BlockSpec((d, tf), lambda i, j: (0, j)),
            pl.BlockSpec((tf, d), lambda i, j: (j, 0)),
        ],
        out_specs=pl.BlockSpec((tm, d), lambda i, j: (i, 0)),
        scratch_shapes=[pltpu.VMEM((tm, d), BF16)],
        compiler_params=_params(2),
        name=f"ffn{sub}",
    )(x, mod, g, wg, wu, wd)


def _proj_kernel(x_ref, mod_ref, g_ref, w_ref, hg_ref, cos_ref, sin_ref, o_ref, *, kinds, n_latent):
    tm, d = x_ref.shape
    is_ctx = _is_ctx_rows(tm, n_latent)
    h = _modulated(x_ref[...], g_ref[...], mod_ref, is_ctx, 1).astype(BF16)
    cosf = cos_ref[...]
    sinf = sin_ref[...]
    n_heads = len(kinds)
    heads_per_dot = 4
    for h0 in range(0, n_heads, heads_per_dot):
        nh = min(heads_per_dot, n_heads - h0)
        p = jnp.dot(h, w_ref[:, h0 * HEAD_DIM:(h0 + nh) * HEAD_DIM], preferred_element_type=F32)
        for i in range(nh):
            hd = h0 + i
            ph = p[:, i * HEAD_DIM:(i + 1) * HEAD_DIM]
            if kinds[hd] != 'v':
                ph = ph * lax.rsqrt(jnp.mean(ph * ph, axis=-1, keepdims=True) + EPS) * hg_ref[hd]
            if kinds[hd] == 'r':
                ph = ph * cosf + pltpu.roll(ph, HEAD_DIM // 2, 1) * sinf
            o_ref[hd] = ph.astype(BF16)


def _proj(x, mod, g, w, head_gains, cosf, sinf, *, kinds, n_latent):
    n_rows, d = x.shape
    n_heads = len(kinds)
    tm = _pick_tile(n_rows, (384, 256, 128))
    return pl.pallas_call(
        functools.partial(_proj_kernel, kinds=kinds, n_latent=n_latent),
        out_shape=jax.ShapeDtypeStruct((n_heads, n_rows, HEAD_DIM), BF16),
        grid=(n_rows // tm,),
        in_specs=[
            pl.BlockSpec((tm, d), lambda i: (i, 0)),
            pl.BlockSpec(mod.shape, lambda i: (0, 0)),
            pl.BlockSpec((1, d), lambda i: (0, 0)),
            pl.BlockSpec(w.shape, lambda i: (0, 0), pipeline_mode=pl.Buffered(1)),
            pl.BlockSpec(head_gains.shape, lambda i: (0, 0, 0)),
            pl.BlockSpec((tm, HEAD_DIM), lambda i: (i, 0)),
            pl.BlockSpec((tm, HEAD_DIM), lambda i: (i, 0)),
        ],
        out_specs=pl.BlockSpec((n_heads, tm, HEAD_DIM), lambda i: (0, i, 0)),
        compiler_params=_params(1),
        name="mixer_in_proj",
    )(x, mod, g, w, head_gains, cosf, sinf)


def _oproj_kernel(*refs, n_parts, n_latent):
    x_ref, mod_ref, w_ref = refs[0], refs[1], refs[2]
    part_refs = refs[3:3 + n_parts]
    o_ref = refs[3 + n_parts]
    tm, d = x_ref.shape
    is_ctx = _is_ctx_rows(tm, n_latent)
    acc = None
    k0 = 0
    for p_ref in part_refs:
        kp = p_ref.shape[1]
        t = jnp.dot(p_ref[...], w_ref[k0:k0 + kp, :], preferred_element_type=F32)
        acc = t if acc is None else acc + t
        k0 += kp
    gate = _mod_row(mod_ref, is_ctx, 5, d)
    o_ref[...] = x_ref[...] + gate * acc


def _oproj(x, n_rows, mod, w, parts, *, n_latent):
    d = x.shape[1]
    tm = _pick_tile(n_rows, (384, 512, 256, 128))
    return pl.pallas_call(
        functools.partial(_oproj_kernel, n_parts=len(parts), n_latent=n_latent),
        out_shape=jax.ShapeDtypeStruct((n_rows, d), F32),
        grid=(n_rows // tm,),
        in_specs=[
            pl.BlockSpec((tm, d), lambda i: (i, 0)),
            pl.BlockSpec(mod.shape, lambda i: (0, 0)),
            pl.BlockSpec(w.shape, lambda i: (0, 0), pipeline_mode=pl.Buffered(1)),
        ] + [pl.BlockSpec((tm, p.shape[1]), lambda i: (i, 0)) for p in parts],
        out_specs=pl.BlockSpec((tm, d), lambda i: (i, 0)),
        compiler_params=_params(1),
        name="mixer_out_proj",
    )(x, mod, w, *parts)


def _nt_dot(a, b):
    return lax.dot_general(a, b, (((1,), (1,)), ((), ())), preferred_element_type=F32)


def _softmax_pv(score_blocks, value_blocks, extra_logit=None):
    m = None
    for s in score_blocks:
        bm = jnp.max(s, axis=-1, keepdims=True)
        m = bm if m is None else jnp.maximum(m, bm)
    if extra_logit is not None:
        m = jnp.maximum(m, extra_logit)
    l = None
    o = None
    for s, v in zip(score_blocks, value_blocks):
        p = jnp.exp(s - m)
        bl = jnp.sum(p, axis=-1, keepdims=True)
        l = bl if l is None else l + bl
        t = jnp.dot(p.astype(BF16), v, preferred_element_type=F32)
        o = t if o is None else o + t
    if extra_logit is not None:
        l = l + jnp.exp(extra_logit - m)
    return o / l


def _na_bias_indices(rows):
    n_blk = rows // NA_Q_ROWS
    col = np.arange(GRID_W)
    cs = np.clip(col - NA_WIN_W // 2, 0, GRID_W - NA_WIN_W)

    def block(blk):
        r0 = blk * NA_Q_ROWS
        rs_b = int(np.clip(r0 - NA_WIN_H // 2, 0, rows - NA_K_ROWS))
        r = r0 + np.arange(NA_Q_ROWS)
        rs = np.clip(r - NA_WIN_H // 2, 0, rows - NA_WIN_H)
        kr = rs_b + np.arange(NA_K_ROWS)
        row_ok = (kr[None, :] >= rs[:, None]) & (kr[None, :] < rs[:, None] + NA_WIN_H)
        col_ok = (col[None, :] >= cs[:, None]) & (col[None, :] < cs[:, None] + NA_WIN_W)
        assert np.all(rs >= rs_b) and np.all(rs + NA_WIN_H <= rs_b + NA_K_ROWS)
        dr = kr[None, :] - r[:, None]
        dc = col[None, :] - col[:, None]
        shape = (NA_Q_ROWS, GRID_W, NA_K_ROWS, GRID_W)
        ok = np.broadcast_to(row_ok[:, None, :, None] & col_ok[None, :, None, :], shape)
        dri = np.broadcast_to(np.clip(dr + NA_WIN_H - 1, 0, 2 * NA_WIN_H - 2)[:, None, :, None], shape)
        dci = np.broadcast_to(np.clip(dc + NA_WIN_W - 1, 0, 2 * NA_WIN_W - 2)[None, :, None, :], shape)
        nq, nk = NA_Q_ROWS * GRID_W, NA_K_ROWS * GRID_W
        return dri.reshape(nq, nk), dci.reshape(nq, nk), ok.reshape(nq, nk)

    variants = [block(0), block(1), block(n_blk - 1)]
    for blk in range(1, n_blk - 1):
        for a, b in zip(block(blk), variants[1]):
            assert np.array_equal(a, b)
    return tuple(np.stack([v[i] for v in variants]) for i in range(3))


def _na_kernel(q_ref, k_ref, v_ref, t_ref, o_ref, *, rows, n_latent, ctx_len):
    rb = pl.program_id(1)
    n_blk = rows // NA_Q_ROWS
    nk = NA_K_ROWS * GRID_W
    q = q_ref[...]
    k_ctx = k_ref[n_latent:n_latent + ctx_len, :]
    v_ctx = v_ref[n_latent:n_latent + ctx_len, :]
    s_ctx = _nt_dot(q, k_ctx)

    @pl.when(rb < n_blk)
    def _():
        rs_b = jnp.clip(rb * NA_Q_ROWS - NA_WIN_H // 2, 0, rows - NA_K_ROWS)
        start = pl.multiple_of(rs_b * GRID_W, GRID_W)
        variant = jnp.where(rb == 0, 0, jnp.where(rb == n_blk - 1, 2, 1))
        k_win = k_ref[pl.ds(start, nk), :]
        v_win = v_ref[pl.ds(start, nk), :]
        s_win = _nt_dot(q, k_win) + t_ref[variant]
        o_ref[...] = _softmax_pv([s_win, s_ctx], [v_win, v_ctx]).astype(BF16)

    @pl.when(rb >= n_blk)
    def _():
        o_ref[...] = _softmax_pv([s_ctx], [v_ctx]).astype(BF16)


def _na_attention(p, table, *, q_head0, k_head0, v_head0, n_heads, n_latent, ctx_len):
    n_rows = p.shape[1]
    rows = n_latent // GRID_W
    tq = NA_Q_ROWS * GRID_W
    assert ctx_len % tq == 0 and n_latent % tq == 0
    return pl.pallas_call(
        functools.partial(_na_kernel, rows=rows, n_latent=n_latent, ctx_len=ctx_len),
        out_shape=jax.ShapeDtypeStruct((n_rows, n_heads * HEAD_DIM), BF16),
        grid=(n_heads, n_rows // tq),
        in_specs=[
            pl.BlockSpec((None, tq, HEAD_DIM), lambda h, r: (q_head0 + h, r, 0)),
            pl.BlockSpec((None, n_rows, HEAD_DIM), lambda h, r: (k_head0 + h, 0, 0)),
            pl.BlockSpec((None, n_rows, HEAD_DIM), lambda h, r: (v_head0 + h, 0, 0)),
            pl.BlockSpec((None,) + table.shape[1:], lambda h, r: (h, 0, 0, 0)),
        ],
        out_specs=pl.BlockSpec((tq, HEAD_DIM), lambda h, r: (r, h)),
        compiler_params=_params(2),
        name="neighbourhood_attention",
    )(p, p, p, table)


def _sw_kernel(sink_ref, q_ref, k_ref, v_ref, mask_ref, o_ref, *, group, n_latent, ctx_len):
    kv = pl.program_id(0)
    n = pl.program_id(1)
    nb = n_latent // BLOCK
    nk = 3 * BLOCK
    k_ctx = k_ref[n_latent:n_latent + ctx_len, :]
    v_ctx = v_ref[n_latent:n_latent + ctx_len, :]

    def per_head(fn):
        for g in range(group):
            q = q_ref[g]
            sink = jnp.full((1, 1), sink_ref[kv * group + g], F32)
            o_ref[:, g * HEAD_DIM:(g + 1) * HEAD_DIM] = fn(q, _nt_dot(q, k_ctx), sink).astype(BF16)

    @pl.when(n < nb)
    def _():
        first = jnp.clip(n - 1, 0, nb - 3)
        start = pl.multiple_of(first * BLOCK, BLOCK)
        k_win = k_ref[pl.ds(start, nk), :]
        v_win = v_ref[pl.ds(start, nk), :]
        mask = mask_ref[n - first]
        per_head(lambda q, s_ctx, sink: _softmax_pv([_nt_dot(q, k_win) + mask, s_ctx], [v_win, v_ctx], sink))

    @pl.when(n >= nb)
    def _():
        per_head(lambda q, s_ctx, sink: _softmax_pv([s_ctx], [v_ctx], sink))


def _sw_masks():
    i = np.arange(BLOCK)[:, None]
    j = np.arange(3 * BLOCK)[None, :]
    return np.stack([np.where(np.abs(j - v * BLOCK - i) <= SW_RADIUS, 0.0, NEG_INF) for v in range(3)]).astype(np.float32)


def _sw_attention(p, sink, *, q_head0, k_head0, v_head0, n_q_heads, n_kv_heads, n_latent, ctx_len):
    n_rows = p.shape[1]
    group = n_q_heads // n_kv_heads
    assert q_head0 % group == 0 and n_latent // BLOCK >= 3
    masks = jnp.asarray(_sw_masks())
    return pl.pallas_call(
        functools.partial(_sw_kernel, group=group, n_latent=n_latent, ctx_len=ctx_len),
        out_shape=jax.ShapeDtypeStruct((n_rows, n_q_heads * HEAD_DIM), BF16),
        grid=(n_kv_heads, n_rows // BLOCK),
        in_specs=[
            pl.BlockSpec(memory_space=pltpu.SMEM),
            pl.BlockSpec((group, BLOCK, HEAD_DIM), lambda k, n: (q_head0 // group + k, n, 0)),
            pl.BlockSpec((None, n_rows, HEAD_DIM), lambda k, n: (k_head0 + k, 0, 0)),
            pl.BlockSpec((None, n_rows, HEAD_DIM), lambda k, n: (v_head0 + k, 0, 0)),
            pl.BlockSpec(masks.shape, lambda k, n: (0, 0, 0)),
        ],
        out_specs=pl.BlockSpec((BLOCK, group * HEAD_DIM), lambda k, n: (n, k)),
        compiler_params=_params(2),
        name="sliding_window_attention",
    )(sink, p, p, p, masks)


def _dense_kernel(q_ref, k_ref, v_ref, o_ref, *, group, tk):
    n_keys = k_ref.shape[0]
    tq = q_ref.shape[1]
    q = q_ref[...].reshape(group * tq, HEAD_DIM)

    def body(c, carry):
        m, l, acc = carry
        start = pl.multiple_of(c * tk, tk)
        s = _nt_dot(q, k_ref[pl.ds(start, tk), :])
        m_new = jnp.maximum(m, jnp.max(s, axis=-1, keepdims=True))
        alpha = jnp.exp(m - m_new)
        p = jnp.exp(s - m_new)
        l = alpha * l + jnp.sum(p, axis=-1, keepdims=True)
        acc = alpha * acc + jnp.dot(p.astype(BF16), v_ref[pl.ds(start, tk), :], preferred_element_type=F32)
        return m_new, l, acc

    init = (jnp.full((group * tq, 1), NEG_INF, F32), jnp.zeros((group * tq, 1), F32),
            jnp.zeros((group * tq, HEAD_DIM), F32))
    _, l, acc = lax.fori_loop(0, n_keys // tk, body, init)
    o = acc / l
    for g in range(group):
        o_ref[:, g * HEAD_DIM:(g + 1) * HEAD_DIM] = o[g * tq:(g + 1) * tq].astype(BF16)


def _dense_attention(p, *, k_head0, v_head0, n_q_heads, n_kv_heads, n_latent):
    n_rows = p.shape[1]
    group = n_q_heads // n_kv_heads
    tk = _pick_tile(n_rows, (768, 512, 256, 128))
    return pl.pallas_call(
        functools.partial(_dense_kernel, group=group, tk=tk),
        out_shape=jax.ShapeDtypeStruct((n_latent, n_q_heads * HEAD_DIM), BF16),
        grid=(n_kv_heads, n_latent // BLOCK),
        in_specs=[
            pl.BlockSpec((group, BLOCK, HEAD_DIM), lambda k, n: (k, n, 0)),
            pl.BlockSpec((None, n_rows, HEAD_DIM), lambda k, n: (k_head0 + k, 0, 0)),
            pl.BlockSpec((None, n_rows, HEAD_DIM), lambda k, n: (v_head0 + k, 0, 0)),
        ],
        out_specs=pl.BlockSpec((BLOCK, group * HEAD_DIM), lambda k, n: (n, k)),
        compiler_params=_params(2),
        name="dense_gqa_attention",
    )(p, p, p)


def _rope_tables(n_latent, ctx_len):
    t = jnp.arange(n_latent)
    row = (t // GRID_W).astype(F32)
    col = (t % GRID_W).astype(F32)
    axis_dim = HEAD_DIM // 2
    inv = ROPE_THETA ** (-jnp.arange(0, axis_dim, 2, dtype=F32) / axis_dim)
    ang = jnp.concatenate([row[:, None] * inv, col[:, None] * inv], axis=-1)
    cos, sin = jnp.cos(ang), jnp.sin(ang)
    cosf = jnp.concatenate([cos, cos], axis=-1)
    sinf = jnp.concatenate([-sin, sin], axis=-1)
    cosf = jnp.concatenate([cosf, jnp.ones((ctx_len, HEAD_DIM), F32)], axis=0)
    sinf = jnp.concatenate([sinf, jnp.zeros((ctx_len, HEAD_DIM), F32)], axis=0)
    return cosf, sinf


def _head_gains(segments):
    rows = []
    for n, gain, scale in segments:
        g = jnp.ones((HEAD_DIM,), F32) if gain is None else gain.astype(F32) * scale
        rows.append(jnp.broadcast_to(g, (n, 1, HEAD_DIM)))
    return jnp.concatenate(rows, axis=0)


@jax.jit
def kernel(x, c, ctx, c_ctx, adaln_w, adaln_b, norm_g, ffn_w_gate, ffn_w_up, ffn_w_down, ab_w_in, ab_w_out,
           na_q_gain, na_k_gain, na_rel_bias, sw_q_gain, sw_k_gain, sw_sink, gqa_w_in, gqa_w_out,
           gqa_q_gain, gqa_k_gain):
    batch, n_latent, d = x.shape
    ctx_len = ctx.shape[1]
    depth = adaln_w.shape[0]
    assert batch == 1 and depth == 2
    n_heads = d // HEAD_DIM
    na_heads = n_heads // 2
    sw_heads = n_heads - na_heads
    n_all = n_latent + ctx_len

    mods = _adaln(jnp.stack([c[0], c_ctx], axis=-1), adaln_w, adaln_b)
    cosf, sinf = _rope_tables(n_latent, ctx_len)
    xs = jnp.concatenate([x[0], ctx[0]], axis=0)

    wg = ffn_w_gate.astype(BF16)
    wu = ffn_w_up.astype(BF16)
    wd = ffn_w_down.astype(BF16)

    def ffn(xs, n_rows, layer, which):
        sub = 0 if which == 0 else 2
        return _ffn(xs, n_rows, mods[layer], norm_g[layer, sub][None], wg[layer, which], wu[layer, which],
                    wd[layer, which], sub=sub, n_latent=n_latent)

    xs = ffn(xs, n_all, 0, 0)
    kinds0 = ('n',) * na_heads + ('r',) * sw_heads + ('n',) * na_heads + ('v',) * na_heads \
        + ('r',) * SW_KV_HEADS + ('v',) * SW_KV_HEADS
    gains0 = _head_gains([(na_heads, na_q_gain[0], ATTN_SCALE), (sw_heads, sw_q_gain[0], ATTN_SCALE),
                          (na_heads, na_k_gain[0], 1.0), (na_heads, None, 1.0),
                          (SW_KV_HEADS, sw_k_gain[0], 1.0), (SW_KV_HEADS, None, 1.0)])
    p0 = _proj(xs, mods[0], norm_g[0, 1][None], ab_w_in[0].astype(BF16), gains0, cosf, sinf,
               kinds=kinds0, n_latent=n_latent)
    dri, dci, ok = _na_bias_indices(n_latent // GRID_W)
    table = jnp.where(ok[None], na_rel_bias[0][:, dri, dci], NEG_INF).astype(F32)
    ka0 = na_heads + sw_heads
    va0 = ka0 + na_heads
    kb0 = va0 + na_heads
    vb0 = kb0 + SW_KV_HEADS
    o_a = _na_attention(p0, table, q_head0=0, k_head0=ka0, v_head0=va0, n_heads=na_heads,
                        n_latent=n_latent, ctx_len=ctx_len)
    o_b = _sw_attention(p0, sw_sink[0].astype(F32), q_head0=na_heads, k_head0=kb0, v_head0=vb0,
                        n_q_heads=sw_heads, n_kv_heads=SW_KV_HEADS, n_latent=n_latent, ctx_len=ctx_len)
    xs = _oproj(xs, n_all, mods[0], ab_w_out[0].astype(BF16), [o_a, o_b], n_latent=n_latent)
    xs = ffn(xs, n_all, 0, 1)

    xs = ffn(xs, n_all, 1, 0)
    kinds1 = ('r',) * n_heads + ('r',) * GQA_KV_HEADS + ('v',) * GQA_KV_HEADS
    gains1 = _head_gains([(n_heads, gqa_q_gain[0], ATTN_SCALE), (GQA_KV_HEADS, gqa_k_gain[0], 1.0),
                          (GQA_KV_HEADS, None, 1.0)])
    p1 = _proj(xs, mods[1], norm_g[1, 1][None], gqa_w_in[0].astype(BF16), gains1, cosf, sinf,
               kinds=kinds1, n_latent=n_latent)
    o_c = _dense_attention(p1, k_head0=n_heads, v_head0=n_heads + GQA_KV_HEADS, n_q_heads=n_heads,
                           n_kv_heads=GQA_KV_HEADS, n_latent=n_latent)
    xl = _oproj(xs, n_latent, mods[1], gqa_w_out[0].astype(BF16), [o_c], n_latent=n_latent)
    xl = ffn(xl, n_latent, 1, 1)
    return xl[None]
```

```python
import functools
import math

import numpy as np
import jax
import jax.numpy as jnp
from jax import lax
from jax.experimental import pallas as pl
from jax.experimental.pallas import tpu as pltpu

F32 = jnp.float32
BF16 = jnp.bfloat16

HEAD_DIM = 128
GRID_W = 64
NA_WIN_H = 8
NA_WIN_W = 16
SW_RADIUS = 128
BLOCK = 128
SW_KV_HEADS = 2
GQA_KV_HEADS = 4
N_MOD = 9
EPS = 1e-6
NEG_INF = -1e30
ROPE_THETA = 10000.0
MACARON_WEIGHT = 0.5
ATTN_SCALE = HEAD_DIM ** -0.5
LOG2_E = math.log2(math.e)

V7X_VMEM_BYTES = 64 * 1024 * 1024
VMEM_LIMIT_BYTES = 60000 * 1024

ROW_CHUNK = 128
COL_CHUNK = 512

NA_Q_ROWS = 4
NA_K_ROWS = NA_Q_ROWS + NA_WIN_H - 1


def _params(n_axes):
    return pltpu.CompilerParams(dimension_semantics=("arbitrary",) * n_axes,
                                vmem_limit_bytes=VMEM_LIMIT_BYTES)


def _pick_tile(n, candidates):
    for t in candidates:
        if n % t == 0:
            return t
    raise ValueError(f"no tile for {n} among {candidates}")


def _adaln_kernel(c_ref, w_ref, b_ref, o_ref):
    d = w_ref.shape[0]
    tn = w_ref.shape[1]

    def body(kc, acc):
        a0, a1 = acc
        r = pl.multiple_of(kc * 8, 8)
        w = w_ref[pl.ds(r, 8), :]
        cv = c_ref[pl.ds(r, 8), :]
        s = cv * jax.nn.sigmoid(cv)
        return a0 + s[:, 0:1] * w, a1 + s[:, 1:2] * w

    z = jnp.zeros((8, tn), F32)
    a0, a1 = lax.fori_loop(0, d // 8, body, (z, z), unroll=8)
    o_ref[0:1, :] = jnp.sum(a0, axis=0, keepdims=True) + b_ref[...]
    o_ref[1:2, :] = jnp.sum(a1, axis=0, keepdims=True) + b_ref[...]


def _adaln(c_cols, adaln_w, adaln_b):
    depth, d, n = adaln_w.shape
    tn = _pick_tile(n, (1024, 512, 256, 128))
    return pl.pallas_call(
        _adaln_kernel,
        out_shape=jax.ShapeDtypeStruct((depth, 2, n), F32),
        grid=(depth, n // tn),
        in_specs=[
            pl.BlockSpec((d, 2), lambda l, j: (0, 0)),
            pl.BlockSpec((None, d, tn), lambda l, j: (l, 0, j)),
            pl.BlockSpec((None, 1, tn), lambda l, j: (l, 0, j)),
        ],
        out_specs=pl.BlockSpec((None, 2, tn), lambda l, j: (l, 0, j)),
        compiler_params=_params(2),
        name="adaln",
    )(c_cols, adaln_w, adaln_b.reshape(depth, 1, n))


def _is_ctx_rows(tm, n_latent, r0=0, nr=None):
    nr = tm if nr is None else nr
    row = pl.program_id(0) * tm + r0 + lax.broadcasted_iota(jnp.int32, (nr, 1), 0)
    return row >= n_latent


def _mod_row(mod_ref, is_ctx, idx, d):
    lat = mod_ref[0:1, idx * d:(idx + 1) * d]
    ctx = mod_ref[1:2, idx * d:(idx + 1) * d]
    return jnp.where(is_ctx, ctx, lat)


def _modulated(x, g, mod_ref, is_ctx, sub):
    d = x.shape[1]
    shift = _mod_row(mod_ref, is_ctx, 3 * sub, d)
    scale = _mod_row(mod_ref, is_ctx, 3 * sub + 1, d)
    y = x * lax.rsqrt(jnp.mean(x * x, axis=-1, keepdims=True) + EPS)
    return (y * g) * (1.0 + scale) + shift


def _ffn_kernel(x_ref, mod_ref, g_ref, wg_ref, wu_ref, wd_ref, o_ref, h_ref, *, sub, n_latent):
    j = pl.program_id(1)
    tm, d = x_ref.shape
    row_chunks = [(r0, min(ROW_CHUNK, tm - r0)) for r0 in range(0, tm, ROW_CHUNK)]

    @pl.when(j == 0)
    def _():
        for r0, nr in row_chunks:
            is_ctx = _is_ctx_rows(tm, n_latent, r0, nr)
            h_ref[r0:r0 + nr, :] = _modulated(x_ref[r0:r0 + nr, :], g_ref[...], mod_ref, is_ctx, sub).astype(BF16)
        o_ref[...] = jnp.zeros_like(o_ref)

    h = h_ref[...]
    gate = jnp.dot(h, wg_ref[...], preferred_element_type=F32)
    up = jnp.dot(h, wu_ref[...], preferred_element_type=F32)
    a = (gate * jax.nn.sigmoid(gate) * up).astype(BF16)
    for c0 in range(0, d, COL_CHUNK):
        o_ref[:, c0:c0 + COL_CHUNK] += jnp.dot(a, wd_ref[:, c0:c0 + COL_CHUNK], preferred_element_type=F32)

    @pl.when(j == pl.num_programs(1) - 1)
    def _():
        for r0, nr in row_chunks:
            is_ctx = _is_ctx_rows(tm, n_latent, r0, nr)
            res_gate = _mod_row(mod_ref, is_ctx, 3 * sub + 2, d)
            o_ref[r0:r0 + nr, :] = x_ref[r0:r0 + nr, :] + MACARON_WEIGHT * res_gate * o_ref[r0:r0 + nr, :]


def _ffn(x, n_rows, mod, g, wg, wu, wd, w_index, *, sub, n_latent):
    d = x.shape[1]
    f = wg.shape[-1]
    la, wh = w_index
    tm = _pick_tile(n_rows, (768, 512, 256, 128))
    tf = _pick_tile(f, (512, 256, 128))
    return pl.pallas_call(
        functools.partial(_ffn_kernel, sub=sub, n_latent=n_latent),
        out_shape=jax.ShapeDtypeStruct((n_rows, d), F32),
        grid=(n_rows // tm, f // tf),
        in_specs=[
            pl.BlockSpec((tm, d), lambda i, j: (i, 0)),
            pl.BlockSpec(mod.shape, lambda i, j: (0, 0)),
            pl.BlockSpec((1, d), lambda i, j: (0, 0)),
            pl.BlockSpec((None, None, d, tf), lambda i, j: (la, wh, 0, j)),
            pl.BlockSpec((None, None, d, tf), lambda i, j: (la, wh, 0, j)),
            pl.BlockSpec((None, None, tf, d), lambda i, j: (la, wh, j, 0)),
        ],
        out_specs=pl.BlockSpec((tm, d), lambda i, j: (i, 0)),
        scratch_shapes=[pltpu.VMEM((tm, d), BF16)],
        compiler_params=_params(2),
        name=f"ffn{sub}",
    )(x, mod, g, wg, wu, wd)


def _proj_kernel(x_ref, mod_ref, g_ref, w_ref, hg_ref, cos_ref, sin_ref, *o_refs, groups, n_latent):
    tm, d = x_ref.shape
    is_ctx = _is_ctx_rows(tm, n_latent)
    h = _modulated(x_ref[...], g_ref[...], mod_ref, is_ctx, 1).astype(BF16)
    cosf = cos_ref[...]
    sinf = sin_ref[...]
    head_out = [(o_ref, i, kind, transposed)
                for o_ref, (n, kind, transposed) in zip(o_refs, groups) for i in range(n)]
    n_heads = len(head_out)
    heads_per_dot = 4
    for h0 in range(0, n_heads, heads_per_dot):
        nh = min(heads_per_dot, n_heads - h0)
        p = jnp.dot(h, w_ref[:, h0 * HEAD_DIM:(h0 + nh) * HEAD_DIM], preferred_element_type=F32)
        for i in range(nh):
            o_ref, slot, kind, transposed = head_out[h0 + i]
            ph = p[:, i * HEAD_DIM:(i + 1) * HEAD_DIM]
            if kind != 'v':
                ph = ph * lax.rsqrt(jnp.mean(ph * ph, axis=-1, keepdims=True) + EPS) * hg_ref[h0 + i]
            if kind == 'r':
                ph = ph * cosf + pltpu.roll(ph, HEAD_DIM // 2, 1) * sinf
            o_ref[slot] = (ph.T if transposed else ph).astype(BF16)


def _proj(x, mod, g, w, head_gains, cosf, sinf, *, groups, n_latent):
    n_rows, d = x.shape
    tm = _pick_tile(n_rows, (384, 256, 128))
    out_shape, out_specs = [], []
    for n, _, transposed in groups:
        if transposed:
            out_shape.append(jax.ShapeDtypeStruct((n, HEAD_DIM, n_rows), BF16))
            out_specs.append(pl.BlockSpec((n, HEAD_DIM, tm), lambda i: (0, 0, i)))
        else:
            out_shape.append(jax.ShapeDtypeStruct((n, n_rows, HEAD_DIM), BF16))
            out_specs.append(pl.BlockSpec((n, tm, HEAD_DIM), lambda i: (0, i, 0)))
    return pl.pallas_call(
        functools.partial(_proj_kernel, groups=groups, n_latent=n_latent),
        out_shape=out_shape,
        grid=(n_rows // tm,),
        in_specs=[
            pl.BlockSpec((tm, d), lambda i: (i, 0)),
            pl.BlockSpec(mod.shape, lambda i: (0, 0)),
            pl.BlockSpec((1, d), lambda i: (0, 0)),
            pl.BlockSpec(w.shape, lambda i: (0, 0), pipeline_mode=pl.Buffered(1)),
            pl.BlockSpec(head_gains.shape, lambda i: (0, 0, 0)),
            pl.BlockSpec((tm, HEAD_DIM), lambda i: (i, 0)),
            pl.BlockSpec((tm, HEAD_DIM), lambda i: (i, 0)),
        ],
        out_specs=out_specs,
        compiler_params=_params(1),
        name="mixer_in_proj",
    )(x, mod, g, w, head_gains, cosf, sinf)


def _oproj_kernel(*refs, n_parts, n_latent):
    x_ref, mod_ref, w_ref = refs[0], refs[1], refs[2]
    part_refs = refs[3:3 + n_parts]
    o_ref = refs[3 + n_parts]
    tm, d = x_ref.shape
    is_ctx = _is_ctx_rows(tm, n_latent)
    acc = None
    k0 = 0
    for p_ref in part_refs:
        kp = p_ref.shape[1]
        t = jnp.dot(p_ref[...], w_ref[k0:k0 + kp, :], preferred_element_type=F32)
        acc = t if acc is None else acc + t
        k0 += kp
    gate = _mod_row(mod_ref, is_ctx, 5, d)
    o_ref[...] = x_ref[...] + gate * acc


def _oproj(x, n_rows, mod, w, parts, *, n_latent):
    d = x.shape[1]
    tm = _pick_tile(n_rows, (384, 512, 256, 128))
    return pl.pallas_call(
        functools.partial(_oproj_kernel, n_parts=len(parts), n_latent=n_latent),
        out_shape=jax.ShapeDtypeStruct((n_rows, d), F32),
        grid=(n_rows // tm,),
        in_specs=[
            pl.BlockSpec((tm, d), lambda i: (i, 0)),
            pl.BlockSpec(mod.shape, lambda i: (0, 0)),
            pl.BlockSpec(w.shape, lambda i: (0, 0), pipeline_mode=pl.Buffered(1)),
        ] + [pl.BlockSpec((tm, p.shape[1]), lambda i: (i, 0)) for p in parts],
        out_specs=pl.BlockSpec((tm, d), lambda i: (i, 0)),
        compiler_params=_params(1),
        name="mixer_out_proj",
    )(x, mod, w, *parts)


def _nt_dot(a, b):
    return lax.dot_general(a, b, (((1,), (1,)), ((), ())), preferred_element_type=F32)


def _softmax_pv(score_blocks, value_blocks, extra_logit=None):
    m = None
    for s in score_blocks:
        bm = jnp.max(s, axis=-1, keepdims=True)
        m = bm if m is None else jnp.maximum(m, bm)
    if extra_logit is not None:
        m = jnp.maximum(m, extra_logit)
    l = None
    o = None
    for s, v in zip(score_blocks, value_blocks):
        p = jnp.exp(s - m)
        bl = jnp.sum(p, axis=-1, keepdims=True)
        l = bl if l is None else l + bl
        t = jnp.dot(p.astype(BF16), v, preferred_element_type=F32)
        o = t if o is None else o + t
    if extra_logit is not None:
        l = l + jnp.exp(extra_logit - m)
    return o / l


def _na_bias_indices(rows):
    n_blk = rows // NA_Q_ROWS
    col = np.arange(GRID_W)
    cs = np.clip(col - NA_WIN_W // 2, 0, GRID_W - NA_WIN_W)

    def block(blk):
        r0 = blk * NA_Q_ROWS
        rs_b = int(np.clip(r0 - NA_WIN_H // 2, 0, rows - NA_K_ROWS))
        r = r0 + np.arange(NA_Q_ROWS)
        rs = np.clip(r - NA_WIN_H // 2, 0, rows - NA_WIN_H)
        kr = rs_b + np.arange(NA_K_ROWS)
        row_ok = (kr[None, :] >= rs[:, None]) & (kr[None, :] < rs[:, None] + NA_WIN_H)
        assert np.all(rs >= rs_b) and np.all(rs + NA_WIN_H <= rs_b + NA_K_ROWS)
        dri = np.clip(kr[None, :] - r[:, None] + NA_WIN_H - 1, 0, 2 * NA_WIN_H - 2)
        return dri, row_ok

    variants = [block(0), block(1), block(n_blk - 1)]
    for blk in range(1, n_blk - 1):
        for a, b in zip(block(blk), variants[1]):
            assert np.array_equal(a, b)
    dri = np.stack([v[0] for v in variants])
    row_ok = np.stack([v[1] for v in variants])
    dci = np.clip(col[None, :] - col[:, None] + NA_WIN_W - 1, 0, 2 * NA_WIN_W - 2)
    col_ok = (col[None, :] >= cs[:, None]) & (col[None, :] < cs[:, None] + NA_WIN_W)
    return dri, row_ok, dci, col_ok


def _na_bias_table(rel_bias, rows):
    dri, row_ok, dci, col_ok = _na_bias_indices(rows)
    n_h = rel_bias.shape[0]
    pick_row = np.eye(2 * NA_WIN_H - 1, dtype=np.float32)[dri.reshape(-1)]
    pick_col = np.eye(2 * NA_WIN_W - 1, dtype=np.float32)[dci.reshape(-1)].T
    t = jnp.einsum('ra,hab->hrb', pick_row, rel_bias, precision=lax.Precision.HIGHEST)
    t = jnp.einsum('hrb,bn->hrn', t, pick_col, precision=lax.Precision.HIGHEST)
    t = t.reshape(n_h, 3, NA_Q_ROWS, NA_K_ROWS, GRID_W, GRID_W).transpose(0, 1, 2, 4, 3, 5)
    ok = row_ok[:, :, None, :, None] & col_ok[None, None, :, None, :]
    t = jnp.where(ok[None], t, NEG_INF)
    return t.reshape(n_h, 3, NA_Q_ROWS * GRID_W, NA_K_ROWS * GRID_W)


def _na_kernel(q_ref, k_ref, v_ref, t_ref, o_ref, *, rows, n_latent, ctx_len):
    rb = pl.program_id(1)
    n_blk = rows // NA_Q_ROWS
    nk = NA_K_ROWS * GRID_W
    q = q_ref[...]
    k_ctx = k_ref[n_latent:n_latent + ctx_len, :]
    v_ctx = v_ref[n_latent:n_latent + ctx_len, :]
    s_ctx = _nt_dot(q, k_ctx)

    @pl.when(rb < n_blk)
    def _():
        rs_b = jnp.clip(rb * NA_Q_ROWS - NA_WIN_H // 2, 0, rows - NA_K_ROWS)
        start = pl.multiple_of(rs_b * GRID_W, GRID_W)
        variant = jnp.where(rb == 0, 0, jnp.where(rb == n_blk - 1, 2, 1))
        k_win = k_ref[pl.ds(start, nk), :]
        v_win = v_ref[pl.ds(start, nk), :]
        s_win = _nt_dot(q, k_win) + t_ref[variant]
        o_ref[...] = _softmax_pv([s_win, s_ctx], [v_win, v_ctx]).astype(BF16)

    @pl.when(rb >= n_blk)
    def _():
        o_ref[...] = _softmax_pv([s_ctx], [v_ctx]).astype(BF16)


def _na_attention(q, k, v, table, *, n_latent, ctx_len):
    n_heads, n_rows, _ = q.shape
    rows = n_latent // GRID_W
    tq = NA_Q_ROWS * GRID_W
    assert ctx_len % tq == 0 and n_latent % tq == 0
    return pl.pallas_call(
        functools.partial(_na_kernel, rows=rows, n_latent=n_latent, ctx_len=ctx_len),
        out_shape=jax.ShapeDtypeStruct((n_rows, n_heads * HEAD_DIM), BF16),
        grid=(n_heads, n_rows // tq),
        in_specs=[
            pl.BlockSpec((None, tq, HEAD_DIM), lambda h, r: (h, r, 0)),
            pl.BlockSpec((None, n_rows, HEAD_DIM), lambda h, r: (h, 0, 0)),
            pl.BlockSpec((None, n_rows, HEAD_DIM), lambda h, r: (h, 0, 0)),
            pl.BlockSpec((None,) + table.shape[1:], lambda h, r: (h, 0, 0, 0)),
        ],
        out_specs=pl.BlockSpec((tq, HEAD_DIM), lambda h, r: (r, h)),
        compiler_params=_params(2),
        name="neighbourhood_attention",
    )(q, k, v, table)


def _sw_kernel(sink_ref, q_ref, k_ref, v_ref, mask_ref, o_ref, *, group, n_latent, ctx_len):
    kv = pl.program_id(0)
    n = pl.program_id(1)
    nb = n_latent // BLOCK
    nk = 3 * BLOCK
    k_ctx = k_ref[n_latent:n_latent + ctx_len, :]
    v_ctx = v_ref[n_latent:n_latent + ctx_len, :]

    def per_head(fn):
        for g in range(group):
            q = q_ref[g]
            sink = jnp.full((1, 1), sink_ref[kv * group + g], F32)
            o_ref[:, g * HEAD_DIM:(g + 1) * HEAD_DIM] = fn(q, _nt_dot(q, k_ctx), sink).astype(BF16)

    @pl.when(n < nb)
    def _():
        first = jnp.clip(n - 1, 0, nb - 3)
        start = pl.multiple_of(first * BLOCK, BLOCK)
        k_win = k_ref[pl.ds(start, nk), :]
        v_win = v_ref[pl.ds(start, nk), :]
        mask = mask_ref[n - first]
        per_head(lambda q, s_ctx, sink: _softmax_pv([_nt_dot(q, k_win) + mask, s_ctx], [v_win, v_ctx], sink))

    @pl.when(n >= nb)
    def _():
        per_head(lambda q, s_ctx, sink: _softmax_pv([s_ctx], [v_ctx], sink))


def _sw_masks():
    i = np.arange(BLOCK)[:, None]
    j = np.arange(3 * BLOCK)[None, :]
    return np.stack([np.where(np.abs(j - v * BLOCK - i) <= SW_RADIUS, 0.0, NEG_INF) for v in range(3)]).astype(np.float32)


def _sw_attention(q, k, v, sink, *, n_latent, ctx_len):
    n_q_heads, n_rows, _ = q.shape
    n_kv_heads = k.shape[0]
    group = n_q_heads // n_kv_heads
    assert n_latent // BLOCK >= 3
    masks = jnp.asarray(_sw_masks())
    return pl.pallas_call(
        functools.partial(_sw_kernel, group=group, n_latent=n_latent, ctx_len=ctx_len),
        out_shape=jax.ShapeDtypeStruct((n_rows, n_q_heads * HEAD_DIM), BF16),
        grid=(n_kv_heads, n_rows // BLOCK),
        in_specs=[
            pl.BlockSpec(memory_space=pltpu.SMEM),
            pl.BlockSpec((group, BLOCK, HEAD_DIM), lambda k, n: (k, n, 0)),
            pl.BlockSpec((None, n_rows, HEAD_DIM), lambda k, n: (k, 0, 0)),
            pl.BlockSpec((None, n_rows, HEAD_DIM), lambda k, n: (k, 0, 0)),
            pl.BlockSpec(masks.shape, lambda k, n: (0, 0, 0)),
        ],
        out_specs=pl.BlockSpec((BLOCK, group * HEAD_DIM), lambda k, n: (n, k)),
        compiler_params=_params(2),
        name="sliding_window_attention",
    )(sink, q, k, v, masks)


def _dense_kernel(q_ref, k_ref, v_ref, o_ref, *, group, tk):
    n_keys = k_ref.shape[0]
    tq = q_ref.shape[2]
    cols = group * tq
    qt = jnp.concatenate([q_ref[g] for g in range(group)], axis=1)

    def body(c, carry):
        m, l, acc = carry
        start = pl.multiple_of(c * tk, tk)
        st = jnp.dot(k_ref[pl.ds(start, tk), :], qt, preferred_element_type=F32)
        m_new = jnp.maximum(m, jnp.max(st, axis=0, keepdims=True))
        alpha = jnp.exp2(m - m_new)
        pt = jnp.exp2(st - m_new)
        l = alpha * l + jnp.sum(pt, axis=0, keepdims=True)
        acc = alpha * acc + jnp.dot(v_ref[:, pl.ds(start, tk)], pt.astype(BF16), preferred_element_type=F32)
        return m_new, l, acc

    init = (jnp.full((1, cols), NEG_INF, F32), jnp.zeros((1, cols), F32), jnp.zeros((HEAD_DIM, cols), F32))
    _, l, acc = lax.fori_loop(0, n_keys // tk, body, init, unroll=True)
    ot = acc / l
    for g in range(group):
        o_ref[:, g * HEAD_DIM:(g + 1) * HEAD_DIM] = ot[:, g * tq:(g + 1) * tq].T.astype(BF16)


def _dense_attention(qt, k, vt, *, n_latent):
    n_q_heads = qt.shape[0]
    n_kv_heads, n_rows, _ = k.shape
    group = n_q_heads // n_kv_heads
    tk = _pick_tile(n_rows, (768, 384, 128))
    return pl.pallas_call(
        functools.partial(_dense_kernel, group=group, tk=tk),
        out_shape=jax.ShapeDtypeStruct((n_latent, n_q_heads * HEAD_DIM), BF16),
        grid=(n_kv_heads, n_latent // BLOCK),
        in_specs=[
            pl.BlockSpec((group, HEAD_DIM, BLOCK), lambda k, n: (k, 0, n)),
            pl.BlockSpec((None, n_rows, HEAD_DIM), lambda k, n: (k, 0, 0)),
            pl.BlockSpec((None, HEAD_DIM, n_rows), lambda k, n: (k, 0, 0)),
        ],
        out_specs=pl.BlockSpec((BLOCK, group * HEAD_DIM), lambda k, n: (n, k)),
        compiler_params=_params(2),
        name="dense_gqa_attention",
    )(qt, k, vt)


def _rope_tables(n_latent, ctx_len):
    t = jnp.arange(n_latent)
    row = (t // GRID_W).astype(F32)
    col = (t % GRID_W).astype(F32)
    axis_dim = HEAD_DIM // 2
    inv = ROPE_THETA ** (-jnp.arange(0, axis_dim, 2, dtype=F32) / axis_dim)
    ang = jnp.concatenate([row[:, None] * inv, col[:, None] * inv], axis=-1)
    cos, sin = jnp.cos(ang), jnp.sin(ang)
    cosf = jnp.concatenate([cos, cos], axis=-1)
    sinf = jnp.concatenate([-sin, sin], axis=-1)
    cosf = jnp.concatenate([cosf, jnp.ones((ctx_len, HEAD_DIM), F32)], axis=0)
    sinf = jnp.concatenate([sinf, jnp.zeros((ctx_len, HEAD_DIM), F32)], axis=0)
    return cosf, sinf


def _head_gains(segments):
    rows = []
    for n, gain, scale in segments:
        g = jnp.ones((HEAD_DIM,), F32) if gain is None else gain.astype(F32) * scale
        rows.append(jnp.broadcast_to(g, (n, 1, HEAD_DIM)))
    return jnp.concatenate(rows, axis=0)


@jax.jit
def kernel(x, c, ctx, c_ctx, adaln_w, adaln_b, norm_g, ffn_w_gate, ffn_w_up, ffn_w_down, ab_w_in, ab_w_out,
           na_q_gain, na_k_gain, na_rel_bias, sw_q_gain, sw_k_gain, sw_sink, gqa_w_in, gqa_w_out,
           gqa_q_gain, gqa_k_gain):
    batch, n_latent, d = x.shape
    ctx_len = ctx.shape[1]
    depth = adaln_w.shape[0]
    assert batch == 1 and depth == 2
    n_heads = d // HEAD_DIM
    na_heads = n_heads // 2
    sw_heads = n_heads - na_heads
    n_all = n_latent + ctx_len

    mods = _adaln(jnp.stack([c[0], c_ctx], axis=-1), adaln_w, adaln_b)
    cosf, sinf = _rope_tables(n_latent, ctx_len)
    xs = jnp.concatenate([x[0], ctx[0]], axis=0)

    wg = ffn_w_gate.astype(BF16)
    wu = ffn_w_up.astype(BF16)
    wd = ffn_w_down.astype(BF16)

    def ffn(xs, n_rows, layer, which):
        sub = 0 if which == 0 else 2
        return _ffn(xs, n_rows, mods[layer], norm_g[layer, sub][None], wg, wu, wd, (layer, which),
                    sub=sub, n_latent=n_latent)

    xs = ffn(xs, n_all, 0, 0)
    groups0 = ((na_heads, 'n', False), (sw_heads, 'r', False), (na_heads, 'n', False), (na_heads, 'v', False),
               (SW_KV_HEADS, 'r', False), (SW_KV_HEADS, 'v', False))
    gains0 = _head_gains([(na_heads, na_q_gain[0], ATTN_SCALE), (sw_heads, sw_q_gain[0], ATTN_SCALE),
                          (na_heads, na_k_gain[0], 1.0), (na_heads, None, 1.0),
                          (SW_KV_HEADS, sw_k_gain[0], 1.0), (SW_KV_HEADS, None, 1.0)])
    qa, qb, ka, va, kb, vb = _proj(xs, mods[0], norm_g[0, 1][None], ab_w_in[0].astype(BF16), gains0, cosf, sinf,
                                   groups=groups0, n_latent=n_latent)
    table = _na_bias_table(na_rel_bias[0].astype(F32), n_latent // GRID_W)
    o_a = _na_attention(qa, ka, va, table, n_latent=n_latent, ctx_len=ctx_len)
    o_b = _sw_attention(qb, kb, vb, sw_sink[0].astype(F32), n_latent=n_latent, ctx_len=ctx_len)
    xs = _oproj(xs, n_all, mods[0], ab_w_out[0].astype(BF16), [o_a, o_b], n_latent=n_latent)
    xs = ffn(xs, n_all, 0, 1)

    xs = ffn(xs, n_all, 1, 0)
    groups1 = ((n_heads, 'r', True), (GQA_KV_HEADS, 'r', False), (GQA_KV_HEADS, 'v', True))
    gains1 = _head_gains([(n_heads, gqa_q_gain[0], ATTN_SCALE * LOG2_E), (GQA_KV_HEADS, gqa_k_gain[0], 1.0),
                          (GQA_KV_HEADS, None, 1.0)])
    qt, kc, vt = _proj(xs, mods[1], norm_g[1, 1][None], gqa_w_in[0].astype(BF16), gains1, cosf, sinf,
                       groups=groups1, n_latent=n_latent)
    o_c = _dense_attention(qt, kc, vt, n_latent=n_latent)
    xl = _oproj(xs, n_latent, mods[1], gqa_w_out[0].astype(BF16), [o_c], n_latent=n_latent)
    xl = ffn(xl, n_latent, 1, 1)
    return xl[None]
```
